```python
import math
import jax
import jax.numpy as jnp
from jax import lax
import numpy as np

D_MODEL = 2048
BATCH = 8
SEQ = 2048
DEPTH = 2
DEC_BATCH = 128
DEC_SEQ = 1
PAST_LEN = 2048
PAGE_SIZE = 128

HGRN_HEADS = 8
HGRN_DK = 128
HGRN_DV = 128
RET_HEADS = 4
RET_DK = 256
RET_DV = 256
MOBA_HEADS = 8
MOBA_DIM = 128
MOBA_BLOCK = 256
MOBA_TOPK = 3
MOBA_QCHUNK = 8
DIFF_HEADS = 4
DIFF_DIM = 128
DIFF_DV = 2 * DIFF_DIM
ATT_QBLOCK = 128
REC_CHUNK = 64
ROPE_BASE = 10000.0
MOE_GROUPS = 4
MOE_EXPERTS_PER_GROUP = 8
MOE_EXPERTS = MOE_GROUPS * MOE_EXPERTS_PER_GROUP
MOE_TOPK = 2
MOE_FF = 1024
MOE_BLOCK = 128
RMS_EPS = 1e-6

N_EVEN = (DEPTH + 1) // 2
N_ODD = DEPTH // 2
EVEN_SPLITS = (HGRN_HEADS * HGRN_DK, HGRN_HEADS * HGRN_DK, HGRN_HEADS * HGRN_DV, HGRN_HEADS * HGRN_DV,
               RET_HEADS * RET_DK, RET_HEADS * RET_DK, RET_HEADS * RET_DV, RET_HEADS * RET_DV)
EVEN_IN = sum(EVEN_SPLITS)
EVEN_MIX = HGRN_HEADS * HGRN_DV + RET_HEADS * RET_DV
ODD_SPLITS = (MOBA_HEADS * MOBA_DIM, MOBA_HEADS * MOBA_DIM, MOBA_HEADS * MOBA_DIM,
              DIFF_HEADS * 2 * DIFF_DIM, DIFF_HEADS * 2 * DIFF_DIM, DIFF_HEADS * DIFF_DV)
ODD_IN = sum(ODD_SPLITS)
ODD_MIX = MOBA_HEADS * MOBA_DIM + DIFF_HEADS * DIFF_DV

kernel_name = 'hybrid_hgrn2_retnet_moba_diffattn_hmoe_step'


def rmsnorm(x, g):
    xf = x.astype(jnp.float32)
    y = xf * lax.rsqrt(jnp.mean(xf * xf, axis=-1, keepdims=True) + RMS_EPS)
    return (y * g.astype(jnp.float32)).astype(x.dtype)


def split_cols(a, sizes):
    return jnp.split(a, np.cumsum(sizes)[:-1].tolist(), axis=-1)


def pad_seq(a, pad):
    if pad == 0:
        return a
    return jnp.pad(a, [(0, 0), (0, pad)] + [(0, 0)] * (a.ndim - 2))


def rotary(x, pos):
    half = x.shape[-1] // 2
    inv = ROPE_BASE ** (-jnp.arange(half, dtype=jnp.float32) / half)
    ang = pos.astype(jnp.float32)[:, None] * inv[None, :]
    cos = jnp.cos(ang)[None, :, None, :]
    sin = jnp.sin(ang)[None, :, None, :]
    xf = x.astype(jnp.float32)
    x1, x2 = xf[..., :half], xf[..., half:]
    return jnp.concatenate([x1 * cos - x2 * sin, x1 * sin + x2 * cos], axis=-1).astype(x.dtype)


def gated_recurrence(q, k, v, log_f, s0):
    B, L, H, DK = q.shape
    DV = v.shape[-1]
    c = min(REC_CHUNK, L)
    n = -(-L // c)
    pad = n * c - L

    def prep(a):
        a = pad_seq(a.astype(jnp.float32), pad)
        return a.reshape(B, n, c, H, a.shape[-1]).transpose(1, 0, 3, 2, 4)

    xs = (prep(q), prep(k), prep(v), prep(log_f))
    causal = jnp.tril(jnp.ones((c, c), dtype=bool))[None, None, :, :, None]

    def step(S, inp):
        qc, kc, vc, gc = inp
        b = jnp.cumsum(gc, axis=2)
        inter = jnp.einsum('bhtk,bhkv->bhtv', qc * jnp.exp(b), S)
        rel = jnp.exp(jnp.where(causal, b[:, :, :, None, :] - b[:, :, None, :, :], -jnp.inf))
        att = jnp.einsum('bhtk,bhsk,bhtsk->bhts', qc, kc, rel)
        intra = jnp.einsum('bhts,bhsv->bhtv', att, vc)
        b_end = b[:, :, -1:, :]
        S_new = jnp.exp(b_end[:, :, 0, :])[..., None] * S + jnp.einsum('bhsk,bhsv->bhkv', kc * jnp.exp(b_end - b), vc)
        return S_new, inter + intra

    S, o = lax.scan(step, s0.astype(jnp.float32), xs)
    o = o.transpose(1, 0, 3, 2, 4).reshape(B, n * c, H, DV)[:, :L]
    return o, S


def even_mixer(h, s_hgrn, s_ret, pos0, w_in, w_out, lb, hgrn_gain, ret_gain):
    B, L, _ = h.shape
    qa, fa, ia, ga, qb, kb, vb, gb = split_cols(h @ w_in, EVEN_SPLITS)
    z = fa.reshape(B, L, HGRN_HEADS, HGRN_DK).astype(jnp.float32)
    lb = lb.reshape(HGRN_HEADS, HGRN_DK)
    log_f = jnp.logaddexp(jnp.log(lb), jnp.log1p(-lb) + jax.nn.log_sigmoid(z))
    k_in = (1.0 - lb) * jax.nn.sigmoid(-z)
    if s_hgrn is None:
        s_hgrn = jnp.zeros((B, HGRN_HEADS, HGRN_DK, HGRN_DV), jnp.float32)
    o_a, sa = gated_recurrence(qa.reshape(B, L, HGRN_HEADS, HGRN_DK), k_in,
                               ia.reshape(B, L, HGRN_HEADS, HGRN_DV), log_f, s_hgrn)
    o_a = rmsnorm(o_a, hgrn_gain) * jax.nn.silu(ga.reshape(B, L, HGRN_HEADS, HGRN_DV).astype(jnp.float32))
    pos = pos0 + jnp.arange(L)
    q_r = rotary(qb.reshape(B, L, RET_HEADS, RET_DK), pos)
    k_r = rotary(kb.reshape(B, L, RET_HEADS, RET_DK), pos) * (RET_DK ** -0.5)
    log_gamma = jnp.log1p(-jnp.exp2(-5.0 - jnp.arange(RET_HEADS, dtype=jnp.float32)))
    log_g = jnp.broadcast_to(log_gamma[None, None, :, None], (B, L, RET_HEADS, RET_DK))
    if s_ret is None:
        s_ret = jnp.zeros((B, RET_HEADS, RET_DK, RET_DV), jnp.float32)
    o_b, sb = gated_recurrence(q_r, k_r, vb.reshape(B, L, RET_HEADS, RET_DV), log_g, s_ret)
    o_b = rmsnorm(o_b, ret_gain) * jax.nn.silu(gb.reshape(B, L, RET_HEADS, RET_DV).astype(jnp.float32))
    mix = jnp.concatenate([o_a.reshape(B, L, -1), o_b.reshape(B, L, -1)], axis=-1).astype(h.dtype)
    return mix @ w_out, sa.astype(h.dtype), sb.astype(h.dtype)


def moba_attention(q, k, v, pos0):
    B, L, H, d = q.shape
    T = k.shape[1]
    scale = d ** -0.5
    nblk = -(-T // MOBA_BLOCK)
    padk = nblk * MOBA_BLOCK - T
    kb = pad_seq(k, padk).reshape(B, nblk, MOBA_BLOCK, H, d).transpose(0, 3, 1, 2, 4)
    vb = pad_seq(v, padk).reshape(B, nblk, MOBA_BLOCK, H, d).transpose(0, 3, 1, 2, 4)
    kmean = jnp.mean(kb.astype(jnp.float32), axis=3)
    n_cand = max(nblk, MOBA_TOPK)
    if n_cand > nblk:
        kmean = jnp.pad(kmean, ((0, 0), (0, 0), (0, n_cand - nblk), (0, 0)))
    qc = min(MOBA_QCHUNK, L)
    n = -(-L // qc)
    qs = pad_seq(q, n * qc - L).reshape(B, n, qc, H, d).transpose(1, 0, 3, 2, 4)
    bi = jnp.arange(B)[:, None, None, None]
    hi = jnp.arange(H)[None, :, None, None]
    offs = jnp.arange(MOBA_BLOCK)

    def one(args):
        qi, ci = args
        pos = pos0 + ci * qc + jnp.arange(qc)
        own = jnp.minimum(pos // MOBA_BLOCK, nblk - 1)
        gate = jnp.einsum('bhqd,bhnd->bhqn', qi.astype(jnp.float32), kmean)
        cand = jnp.arange(n_cand)[None, :] < own[:, None]
        gate = jnp.where(cand, gate, -jnp.inf)
        _, top = lax.top_k(gate, MOBA_TOPK)
        sel_ok = top < own[:, None]
        idx = jnp.concatenate([top, jnp.broadcast_to(own[:, None], (B, H, qc, 1))], axis=-1)
        idx = jnp.minimum(idx, nblk - 1)
        kg = kb[bi, hi, idx]
        vg = vb[bi, hi, idx]
        s = jnp.einsum('bhqd,bhqnkd->bhqnk', qi, kg).astype(jnp.float32) * scale
        slot_ok = jnp.concatenate([sel_ok, jnp.ones((B, H, qc, 1), dtype=bool)], axis=-1)
        key_pos = idx[..., None] * MOBA_BLOCK + offs
        ok = slot_ok[..., None] & (key_pos <= pos[:, None, None])
        s = jnp.where(ok, s, -jnp.inf).reshape(B, H, qc, -1)
        p = jax.nn.softmax(s, axis=-1).reshape(B, H, qc, MOBA_TOPK + 1, MOBA_BLOCK)
        return jnp.einsum('bhqnk,bhqnkd->bhqd', p.astype(v.dtype), vg)

    o = lax.map(one, (qs, jnp.arange(n)))
    return o.transpose(1, 0, 3, 2, 4).reshape(B, n * qc, H, d)[:, :L]


def diff_attention(q, k, v, pos0, lam):
    B, L, H, _, d = q.shape
    T = k.shape[1]
    qb = min(ATT_QBLOCK, L)
    n = -(-L // qb)
    qs = pad_seq(q * (d ** -0.5), n * qb - L).reshape(B, n, qb, H, 2, d).transpose(1, 0, 2, 3, 4, 5)
    key_pos = jnp.arange(T)

    def one(args):
        qi, ci = args
        pos = pos0 + ci * qb + jnp.arange(qb)
        s = jnp.einsum('bqhmd,bkhmd->bhmqk', qi, k).astype(jnp.float32)
        s = jnp.where(key_pos[None, :] <= pos[:, None], s, -jnp.inf)
        p = jax.nn.softmax(s, axis=-1)
        a = p[:, :, 0] - lam * p[:, :, 1]
        return jnp.einsum('bhqk,bkhv->bqhv', a.astype(v.dtype), v)

    o = lax.map(one, (qs, jnp.arange(n)))
    return o.transpose(1, 0, 2, 3, 4).reshape(B, n * qb, H, -1)[:, :L]


def odd_mixer(h, past, pos0, w_in, w_out, mq_gain, mk_gain, dq_gain, dk_gain, lam, lam_init, diff_gain):
    B, L, _ = h.shape
    qc, kc, vc, qd, kd, vd = split_cols(h @ w_in, ODD_SPLITS)
    qc = rmsnorm(qc.reshape(B, L, MOBA_HEADS, MOBA_DIM), mq_gain)
    kc = rmsnorm(kc.reshape(B, L, MOBA_HEADS, MOBA_DIM), mk_gain)
    vc = vc.reshape(B, L, MOBA_HEADS, MOBA_DIM)
    qd = rmsnorm(qd.reshape(B, L, DIFF_HEADS, 2, DIFF_DIM), dq_gain)
    kd = rmsnorm(kd.reshape(B, L, DIFF_HEADS, 2, DIFF_DIM), dk_gain)
    vd = vd.reshape(B, L, DIFF_HEADS, DIFF_DV)
    if past is None:
        kc_all, vc_all, kd_all, vd_all = kc, vc, kd, vd
    else:
        kc_all = jnp.concatenate([past[0].astype(kc.dtype), kc], axis=1)
        vc_all = jnp.concatenate([past[1].astype(vc.dtype), vc], axis=1)
        kd_all = jnp.concatenate([past[2].astype(kd.dtype), kd], axis=1)
        vd_all = jnp.concatenate([past[3].astype(vd.dtype), vd], axis=1)
    o_c = moba_attention(qc * (MOBA_DIM ** -0.5) / (MOBA_DIM ** -0.5), kc_all, vc_all, pos0)
    o_d = diff_attention(qd, kd_all, vd_all, pos0, lam)
    o_d = rmsnorm(o_d, diff_gain) * (1.0 - lam_init)
    mix = jnp.concatenate([o_c.reshape(B, L, -1), o_d.reshape(B, L, -1)], axis=-1).astype(h.dtype)
    return mix @ w_out, kc, vc, kd, vd


def gather_pages(pool, layer, page_table):
    rows = pool[layer, page_table]
    return rows.reshape((page_table.shape[0], -1) + rows.shape[3:])


def grouped_expert_mlp(xf, expert_idx, gates, w_gate, w_up, w_down):
    T, D = xf.shape
    K = expert_idx.shape[1]
    E = w_gate.shape[0]
    A = T * K
    flat_e = expert_idx.reshape(A).astype(jnp.int32)
    flat_tok = jnp.arange(A, dtype=jnp.int32) // K
    flat_g = gates.reshape(A)
    order = jnp.argsort(flat_e)
    se, stok, sg = flat_e[order], flat_tok[order], flat_g[order]
    counts = jnp.bincount(flat_e, length=E).astype(jnp.int32)
    padded = (counts + MOE_BLOCK - 1) // MOE_BLOCK * MOE_BLOCK
    start = jnp.cumsum(counts) - counts
    pend = jnp.cumsum(padded)
    pstart = pend - padded
    dest = pstart[se] + (jnp.arange(A, dtype=jnp.int32) - start[se])
    NB = -(-A // MOE_BLOCK) + E
    P = NB * MOE_BLOCK
    tok_buf = jnp.full((P,), T, jnp.int32).at[dest].set(stok)
    g_buf = jnp.zeros((P,), jnp.float32).at[dest].set(sg.astype(jnp.float32))
    blk_start = jnp.arange(NB, dtype=jnp.int32) * MOE_BLOCK
    blk_e = jnp.minimum(jnp.sum(pend[None, :] <= blk_start[:, None], axis=1), E - 1).astype(jnp.int32)
    x_pad = jnp.concatenate([xf, jnp.zeros((1, D), xf.dtype)], axis=0)

    def blk(args):
        tok, e = args
        xb = x_pad[tok]
        hmid = jax.nn.silu(xb @ w_gate[e]) * (xb @ w_up[e])
        return hmid @ w_down[e]

    out = lax.map(blk, (tok_buf.reshape(NB, MOE_BLOCK), blk_e)).reshape(P, D)
    out = out.astype(jnp.float32) * g_buf[:, None]
    y = jax.ops.segment_sum(out, tok_buf, num_segments=T + 1)[:T]
    return y.astype(xf.dtype)


def hier_moe(h, w_rg, b_rg, w_re, b_re, w_gate, w_up, w_down):
    B, L, D = h.shape
    T = B * L
    xf = h.reshape(T, D)
    g_logits = (xf @ w_rg).astype(jnp.float32) + b_rg.astype(jnp.float32)
    g_sel = jnp.argmax(g_logits, axis=-1).astype(jnp.int32)
    p_g = jnp.take_along_axis(jax.nn.softmax(g_logits, axis=-1), g_sel[:, None], axis=-1)
    e_logits = (xf @ w_re).astype(jnp.float32).reshape(T, MOE_GROUPS, MOE_EXPERTS_PER_GROUP) + b_re.astype(jnp.float32)
    e_in = jnp.take_along_axis(e_logits, g_sel[:, None, None], axis=1)[:, 0]
    top_p, top_i = lax.top_k(jax.nn.softmax(e_in, axis=-1), MOE_TOPK)
    gates = p_g * top_p / jnp.sum(top_p, axis=-1, keepdims=True)
    experts = g_sel[:, None] * MOE_EXPERTS_PER_GROUP + top_i.astype(jnp.int32)
    return grouped_expert_mlp(xf, experts, gates, w_gate, w_up, w_down).reshape(B, L, D)


def setup_inputs(seed: int = 0) -> dict:
    key = jax.random.key(seed)
    keys = list(jax.random.split(key, 48))

    def nrm(shape, scale):
        return jax.random.normal(keys.pop(), shape, jnp.float32) * scale

    def gain(shape):
        return 1.0 + nrm(shape, 0.05)

    n_pages = PAST_LEN // PAGE_SIZE
    used = DEC_BATCH * n_pages
    n_pool = used + used // 4
    page_table = jax.random.permutation(keys.pop(), n_pool)[:used].reshape(DEC_BATCH, n_pages).astype(jnp.int32)
    return {
        'x_prompt': nrm((BATCH, SEQ, D_MODEL), 1.0),
        'x_sample': nrm((DEC_BATCH, DEC_SEQ, D_MODEL), 1.0),
        'state_hgrn': nrm((N_EVEN, DEC_BATCH, HGRN_HEADS, HGRN_DK, HGRN_DV), 0.5),
        'state_ret': nrm((N_EVEN, DEC_BATCH, RET_HEADS, RET_DK, RET_DV), 0.5),
        'cache_moba_k': nrm((N_ODD, n_pool, PAGE_SIZE, MOBA_HEADS, MOBA_DIM), 1.0),
        'cache_moba_v': nrm((N_ODD, n_pool, PAGE_SIZE, MOBA_HEADS, MOBA_DIM), 1.0),
        'cache_diff_k': nrm((N_ODD, n_pool, PAGE_SIZE, DIFF_HEADS, 2, DIFF_DIM), 1.0),
        'cache_diff_v': nrm((N_ODD, n_pool, PAGE_SIZE, DIFF_HEADS, DIFF_DV), 1.0),
        'page_table': page_table,
        'norm_mix': gain((DEPTH, D_MODEL)),
        'norm_ffn': gain((DEPTH, D_MODEL)),
        'w_in_even': nrm((N_EVEN, D_MODEL, EVEN_IN), D_MODEL ** -0.5),
        'w_out_even': nrm((N_EVEN, EVEN_MIX, D_MODEL), EVEN_MIX ** -0.5),
        'hgrn_lb_logits': nrm((DEPTH + 1, HGRN_HEADS * HGRN_DK), 0.5),
        'hgrn_out_norm': gain((N_EVEN, HGRN_DV)),
        'ret_out_norm': gain((N_EVEN, RET_DV)),
        'w_in_odd': nrm((N_ODD, D_MODEL, ODD_IN), D_MODEL ** -0.5),
        'w_out_odd': nrm((N_ODD, ODD_MIX, D_MODEL), ODD_MIX ** -0.5),
        'moba_q_norm': gain((N_ODD, MOBA_DIM)),
        'moba_k_norm': gain((N_ODD, MOBA_DIM)),
        'diff_q_norm': gain((N_ODD, DIFF_DIM)),
        'diff_k_norm': gain((N_ODD, DIFF_DIM)),
        'diff_lambda_q1': nrm((N_ODD, DIFF_DIM), 0.1),
        'diff_lambda_k1': nrm((N_ODD, DIFF_DIM), 0.1),
        'diff_lambda_q2': nrm((N_ODD, DIFF_DIM), 0.1),
        'diff_lambda_k2': nrm((N_ODD, DIFF_DIM), 0.1),
        'diff_out_norm': gain((N_ODD, DIFF_DV)),
        'router_group_w': nrm((DEPTH, D_MODEL, MOE_GROUPS), D_MODEL ** -0.5),
        'router_group_b': nrm((DEPTH, MOE_GROUPS), 0.01),
        'router_expert_w': nrm((DEPTH, D_MODEL, MOE_EXPERTS), D_MODEL ** -0.5),
        'router_expert_b': nrm((DEPTH, MOE_GROUPS, MOE_EXPERTS_PER_GROUP), 0.01),
        'expert_w_gate': nrm((DEPTH, MOE_EXPERTS, D_MODEL, MOE_FF), D_MODEL ** -0.5),
        'expert_w_up': nrm((DEPTH, MOE_EXPERTS, D_MODEL, MOE_FF), D_MODEL ** -0.5),
        'expert_w_down': nrm((DEPTH, MOE_EXPERTS, MOE_FF, D_MODEL), MOE_FF ** -0.5),
    }


def reference(x_prompt, x_sample, state_hgrn, state_ret, cache_moba_k, cache_moba_v, cache_diff_k, cache_diff_v,
              page_table, norm_mix, norm_ffn, w_in_even, w_out_even, hgrn_lb_logits, hgrn_out_norm, ret_out_norm,
              w_in_odd, w_out_odd, moba_q_norm, moba_k_norm, diff_q_norm, diff_k_norm,
              diff_lambda_q1, diff_lambda_k1, diff_lambda_q2, diff_lambda_k2, diff_out_norm,
              router_group_w, router_group_b, router_expert_w, router_expert_b,
              expert_w_gate, expert_w_up, expert_w_down):
    past_len = page_table.shape[1] * PAGE_SIZE
    lb_all = jnp.cumsum(jax.nn.softmax(hgrn_lb_logits.astype(jnp.float32), axis=0), axis=0)
    xp, xs = x_prompt, x_sample
    hg_p, hg_s, rt_p, rt_s = [], [], [], []
    mk_p, mk_s, mv_p, mv_s, dk_p, dk_s, dv_p, dv_s = [], [], [], [], [], [], [], []
    for layer in range(DEPTH):
        i = layer // 2
        hp = rmsnorm(xp, norm_mix[layer])
        hs = rmsnorm(xs, norm_mix[layer])
        if layer % 2 == 0:
            wts = (w_in_even[i], w_out_even[i], lb_all[layer], hgrn_out_norm[i], ret_out_norm[i])
            op, sa_p, sb_p = even_mixer(hp, None, None, 0, *wts)
            os_, sa_s, sb_s = even_mixer(hs, state_hgrn[i], state_ret[i], past_len, *wts)
            hg_p.append(sa_p)
            hg_s.append(sa_s)
            rt_p.append(sb_p)
            rt_s.append(sb_s)
        else:
            lam_init = 0.8 - 0.6 * math.exp(-0.3 * layer)
            lam = (jnp.exp(jnp.sum(diff_lambda_q1[i].astype(jnp.float32) * diff_lambda_k1[i].astype(jnp.float32)))
                   - jnp.exp(jnp.sum(diff_lambda_q2[i].astype(jnp.float32) * diff_lambda_k2[i].astype(jnp.float32)))
                   + lam_init)
            wts = (w_in_odd[i], w_out_odd[i], moba_q_norm[i], moba_k_norm[i], diff_q_norm[i], diff_k_norm[i],
                   lam, lam_init, diff_out_norm[i])
            op, kc_p, vc_p, kd_p, vd_p = odd_mixer(hp, None, 0, *wts)
            past = (gather_pages(cache_moba_k, i, page_table), gather_pages(cache_moba_v, i, page_table),
                    gather_pages(cache_diff_k, i, page_table), gather_pages(cache_diff_v, i, page_table))
            os_, kc_s, vc_s, kd_s, vd_s = odd_mixer(hs, past, past_len, *wts)
            mk_p.append(kc_p)
            mk_s.append(kc_s)
            mv_p.append(vc_p)
            mv_s.append(vc_s)
            dk_p.append(kd_p)
            dk_s.append(kd_s)
            dv_p.append(vd_p)
            dv_s.append(vd_s)
        xp = xp + op.astype(xp.dtype)
        xs = xs + os_.astype(xs.dtype)
        moe = (router_group_w[layer], router_group_b[layer], router_expert_w[layer], router_expert_b[layer],
               expert_w_gate[layer], expert_w_up[layer], expert_w_down[layer])
        xp = xp + hier_moe(rmsnorm(xp, norm_ffn[layer]), *moe)
        xs = xs + hier_moe(rmsnorm(xs, norm_ffn[layer]), *moe)
    return (xp, xs, jnp.stack(hg_p), jnp.stack(hg_s), jnp.stack(rt_p), jnp.stack(rt_s),
            jnp.stack(mk_p), jnp.stack(mk_s), jnp.stack(mv_p), jnp.stack(mv_s),
            jnp.stack(dk_p), jnp.stack(dk_s), jnp.stack(dv_p), jnp.stack(dv_s))
```

```python
import functools
import math

import jax
import jax.numpy as jnp
from jax import lax
from jax.experimental import pallas as pl
from jax.experimental.pallas import tpu as pltpu

F32 = jnp.float32
BF16 = jnp.bfloat16

LANES = 128
RMS_EPS = 1e-6
ROPE_BASE = 10000.0
PAGE_SIZE = 128
HGRN_HEADS, HGRN_DK, HGRN_DV = 8, 128, 128
RET_HEADS, RET_DK, RET_DV = 4, 256, 256
MOBA_HEADS, MOBA_DIM, MOBA_BLOCK, MOBA_TOPK = 8, 128, 256, 3
DIFF_HEADS, DIFF_DIM = 4, 128
DIFF_DV = 2 * DIFF_DIM
MOE_GROUPS, MOE_EXPERTS_PER_GROUP, MOE_TOPK = 4, 8, 2
MOE_EXPERTS = MOE_GROUPS * MOE_EXPERTS_PER_GROUP
MOE_BLOCK = 128
NEG = -1e30
VMEM_LIMIT = 48 * 1024 * 1024


def _params(sem, vmem=VMEM_LIMIT):
    return pltpu.CompilerParams(dimension_semantics=sem, vmem_limit_bytes=vmem)


def _dot(a, b):
    return jnp.dot(a, b, preferred_element_type=F32)


def _dot_nt(a, b):
    return lax.dot_general(a, b, (((1,), (1,)), ((), ())), preferred_element_type=F32)


def _dot_tn(a, b):
    return lax.dot_general(a, b, (((0,), (0,)), ((), ())), preferred_element_type=F32)


def _split2(x):
    hi = x.astype(BF16)
    lo = (x - hi.astype(F32)).astype(BF16)
    return hi, lo


def _split3(x):
    hi = x.astype(BF16)
    r = x - hi.astype(F32)
    mid = r.astype(BF16)
    lo = (r - mid.astype(F32)).astype(BF16)
    return hi, mid, lo


def _dot3(a, b, dot=_dot):
    ah, al = _split2(a)
    bh, bl = _split2(b)
    return dot(ah, bh) + (dot(al, bh) + dot(ah, bl))


def _row_tile(n, cap):
    if n <= cap:
        return n
    best = None
    for t in range(16, cap + 1, 16):
        if n % t == 0:
            best = t
    assert best is not None, (n, cap)
    return best


def _rms(x, eps=RMS_EPS):
    return x * lax.rsqrt(jnp.mean(x * x, axis=-1, keepdims=True) + eps)


def _silu(x):
    return x * jax.nn.sigmoid(x)


def _rmsnorm_kernel(x_ref, g_ref, o_ref):
    o_ref[...] = (_rms(x_ref[...]) * g_ref[...]).astype(o_ref.dtype)


def rmsnorm_rows(x, gain, out_dtype=BF16):
    t, d = x.shape
    tm = _row_tile(t, 1024)
    return pl.pallas_call(
        _rmsnorm_kernel,
        grid=(t // tm,),
        in_specs=[pl.BlockSpec((tm, d), lambda i: (i, 0)), pl.BlockSpec((1, d), lambda i: (0, 0))],
        out_specs=pl.BlockSpec((tm, d), lambda i: (i, 0)),
        out_shape=jax.ShapeDtypeStruct((t, d), out_dtype),
        compiler_params=_params(("parallel",)),
        name="rmsnorm_rows",
    )(x, gain.reshape(1, d))


def _mm_kernel(x_ref, w_ref, o_ref):
    o_ref[...] = _dot(x_ref[...], w_ref[...]).astype(o_ref.dtype)


def _mm_res_kernel(x_ref, w_ref, r_ref, o_ref):
    o_ref[...] = r_ref[...] + _dot(x_ref[...], w_ref[...])


def matmul(x, w, residual=None, tn=1024, tm_cap=1024):
    t, k = x.shape
    n = w.shape[1]
    tn = min(tn, n)
    tm = _row_tile(t, tm_cap)
    in_specs = [pl.BlockSpec((tm, k), lambda j, i: (i, 0)), pl.BlockSpec((k, tn), lambda j, i: (0, j))]
    args = [x, w]
    body = _mm_kernel
    if residual is not None:
        in_specs.append(pl.BlockSpec((tm, tn), lambda j, i: (i, j)))
        args.append(residual)
        body = _mm_res_kernel
    return pl.pallas_call(
        body,
        grid=(n // tn, t // tm),
        in_specs=in_specs,
        out_specs=pl.BlockSpec((tm, tn), lambda j, i: (i, j)),
        out_shape=jax.ShapeDtypeStruct((t, n), F32),
        compiler_params=_params(("parallel", "parallel")),
        name="matmul",
    )(*args)


def _mm_norm_kernel(flag_ref, x_ref, w_ref, g_ref, o_ref, *, tn):
    acc = _dot(x_ref[...], w_ref[...])
    j = pl.program_id(0)

    @pl.when(flag_ref[j] == 0)
    def _():
        o_ref[...] = acc

    @pl.when(flag_ref[j] != 0)
    def _():
        g = g_ref[...]
        for c in range(tn // LANES):
            sl = slice(c * LANES, (c + 1) * LANES)
            o_ref[:, sl] = _rms(acc[:, sl]) * g


def matmul_headnorm(x, w, flags, gains, tn=1024, tm_cap=1024):
    t, k = x.shape
    n = w.shape[1]
    tm = _row_tile(t, tm_cap)
    nt = n // tn
    return pl.pallas_call(
        functools.partial(_mm_norm_kernel, tn=tn),
        grid_spec=pltpu.PrefetchScalarGridSpec(
            num_scalar_prefetch=1,
            grid=(nt, t // tm),
            in_specs=[
                pl.BlockSpec((tm, k), lambda j, i, f: (i, 0)),
                pl.BlockSpec((k, tn), lambda j, i, f: (0, j)),
                pl.BlockSpec((None, 1, LANES), lambda j, i, f: (j, 0, 0)),
            ],
            out_specs=pl.BlockSpec((tm, tn), lambda j, i, f: (i, j)),
        ),
        out_shape=jax.ShapeDtypeStruct((t, n), F32),
        compiler_params=_params(("parallel", "parallel")),
        name="matmul_headnorm",
    )(flags, x, w, gains.reshape(nt, 1, LANES))


def _hgrn_kernel(q_ref, z_ref, v_ref, g_ref, lb_ref, gain_ref, o_ref, s_out_ref, st_scr, *, tl, chunk, sub):
    l = pl.program_id(2)

    @pl.when(l == 0)
    def _():
        st_scr[...] = jnp.zeros_like(st_scr)

    c = chunk
    nseg = c // sub
    row = lax.broadcasted_iota(jnp.int32, (c, c), 0)
    col = lax.broadcasted_iota(jnp.int32, (c, c), 1)
    causal = col <= row
    cum_mat = jnp.concatenate([jnp.where(causal, 1.0, 0.0), jnp.where(col < (row // sub) * sub, 1.0, 0.0)],
                              axis=0).astype(BF16)
    rsub = lax.broadcasted_iota(jnp.int32, (c, HGRN_DK), 0) // sub
    lb = lb_ref[...]
    gain = gain_ref[...]
    for ci in range(tl // c):
        sl = slice(ci * c, (ci + 1) * c)
        q = q_ref[sl, :]
        z = z_ref[sl, :]
        v = v_ref[sl, :].astype(BF16)
        kin = (1.0 - lb) * jax.nn.sigmoid(-z)
        log_f = jnp.log(1.0 - kin)
        hi, mid, lo = _split3(log_f)
        cs = _dot(cum_mat, hi) + (_dot(cum_mat, mid) + _dot(cum_mat, lo))
        b = cs[:c]
        bs = cs[c:]
        qt = q * jnp.exp(b - bs)
        q_parts, k_parts = [], []
        for i in range(nseg):
            bs_i = bs[i * sub:i * sub + 1, :]
            q_parts.append(jnp.where(rsub == i, qt, 0.0).astype(BF16))
            k_parts.append(jnp.where(rsub <= i, kin * jnp.exp(jnp.where(rsub <= i, bs_i - b, 0.0)), 0.0).astype(BF16))
        att = _dot_nt(jnp.concatenate(q_parts, axis=1), jnp.concatenate(k_parts, axis=1))
        att = jnp.where(causal, att, 0.0).astype(BF16)
        st = st_scr[...]
        o = _dot_nt((q * jnp.exp(b)).astype(BF16), st.astype(BF16)) + _dot(att, v)
        b_end = b[c - 1:c, :]
        st_scr[...] = st * jnp.exp(b_end) + _dot_tn(v, (kin * jnp.exp(b_end - b)).astype(BF16))
        o_ref[sl, :] = (_rms(o) * gain * _silu(g_ref[sl, :])).astype(o_ref.dtype)

    @pl.when(l == pl.num_programs(2) - 1)
    def _():
        s_out_ref[...] = st_scr[...].T


def hgrn_prompt(proj, lb, gain, batch, seq, *, tl=512, chunk=64, sub=16):
    h, dk, dv = HGRN_HEADS, HGRN_DK, HGRN_DV
    tl = min(tl, seq)
    nl = seq // tl

    def cols(group):
        return pl.BlockSpec((tl, dk), lambda b, hh, l: (b * nl + l, group * h + hh))

    return pl.pallas_call(
        functools.partial(_hgrn_kernel, tl=tl, chunk=min(chunk, tl), sub=sub),
        grid=(batch, h, nl),
        in_specs=[cols(0), cols(1), cols(2), cols(3),
                  pl.BlockSpec((None, 1, dk), lambda b, hh, l: (hh, 0, 0)),
                  pl.BlockSpec((1, dv), lambda b, hh, l: (0, 0))],
        out_specs=[pl.BlockSpec((tl, dv), lambda b, hh, l: (b * nl + l, hh)),
                   pl.BlockSpec((None, None, dk, dv), lambda b, hh, l: (b, hh, 0, 0))],
        out_shape=[jax.ShapeDtypeStruct((batch * seq, h * dv), BF16),
                   jax.ShapeDtypeStruct((batch, h, dk, dv), F32)],
        scratch_shapes=[pltpu.VMEM((dv, dk), F32)],
        compiler_params=_params(("parallel", "parallel", "arbitrary")),
        name="hgrn_prompt",
    )(proj, proj, proj, proj, lb.reshape(h, 1, dk), gain.reshape(1, dv))


def _ret_kernel(q_ref, k_ref, v_ref, g_ref, cos_ref, sin_ref, dq_ref, dk_ref, dm_ref, gc_ref, gain_ref,
                o_ref, s_out_ref, st_scr):
    l = pl.program_id(2)

    @pl.when(l == 0)
    def _():
        st_scr[...] = jnp.zeros_like(st_scr)

    half = RET_DK // 2
    cos = cos_ref[...]
    sin = sin_ref[...]

    def rot(x):
        x1, x2 = x[:, :half], x[:, half:]
        return jnp.concatenate([x1 * cos - x2 * sin, x1 * sin + x2 * cos], axis=1)

    qr = rot(q_ref[...])
    kr = rot(k_ref[...]) * (RET_DK ** -0.5)
    v = v_ref[...].astype(BF16)
    att = (_dot_nt(qr.astype(BF16), kr.astype(BF16)) * dm_ref[...]).astype(BF16)
    st = st_scr[...]
    o = _dot_nt((qr * dq_ref[...]).astype(BF16), st.astype(BF16)) + _dot(att, v)
    st_scr[...] = st * gc_ref[...] + _dot_tn(v, (kr * dk_ref[...]).astype(BF16))
    o_ref[...] = (_rms(o) * gain_ref[...] * _silu(g_ref[...])).astype(o_ref.dtype)

    @pl.when(l == pl.num_programs(2) - 1)
    def _():
        s_out_ref[...] = st_scr[...].T


def _ret_log_gamma():
    return jnp.log1p(-jnp.exp2(-5.0 - jnp.arange(RET_HEADS, dtype=F32)))


def _rope_tables(pos, half):
    inv = ROPE_BASE ** (-jnp.arange(half, dtype=F32) / half)
    ang = pos.astype(F32)[:, None] * inv[None, :]
    return jnp.cos(ang), jnp.sin(ang)


def ret_prompt(proj, gain, batch, seq, col0, *, chunk=256):
    h, dk, dv = RET_HEADS, RET_DK, RET_DV
    c = min(chunk, seq)
    nl = seq // c
    base = col0 // dk
    lg = _ret_log_gamma()[:, None]
    t = jnp.arange(c, dtype=F32)
    dq = jnp.exp(lg * (t + 1.0))[:, :, None]
    dkk = jnp.exp(lg * (c - 1.0 - t))[:, :, None]
    diff = t[:, None] - t[None, :]
    dm = jnp.where(diff >= 0, jnp.exp(lg[:, :, None] * jnp.maximum(diff, 0.0)), 0.0)
    gc = jnp.exp(lg * c)[:, :, None]
    cos, sin = _rope_tables(jnp.arange(seq), dk // 2)

    def cols(group):
        return pl.BlockSpec((c, dk), lambda b, hh, l: (b * nl + l, base + group * h + hh))

    tab = pl.BlockSpec((c, dk // 2), lambda b, hh, l: (l, 0))
    return pl.pallas_call(
        _ret_kernel,
        grid=(batch, h, nl),
        in_specs=[cols(0), cols(1), cols(2), cols(3), tab, tab,
                  pl.BlockSpec((None, c, 1), lambda b, hh, l: (hh, 0, 0)),
                  pl.BlockSpec((None, c, 1), lambda b, hh, l: (hh, 0, 0)),
                  pl.BlockSpec((None, c, c), lambda b, hh, l: (hh, 0, 0)),
                  pl.BlockSpec((None, 1, 1), lambda b, hh, l: (hh, 0, 0)),
                  pl.BlockSpec((1, dv), lambda b, hh, l: (0, 0))],
        out_specs=[pl.BlockSpec((c, dv), lambda b, hh, l: (b * nl + l, hh)),
                   pl.BlockSpec((None, None, dk, dv), lambda b, hh, l: (b, hh, 0, 0))],
        out_shape=[jax.ShapeDtypeStruct((batch * seq, h * dv), BF16),
                   jax.ShapeDtypeStruct((batch, h, dk, dv), F32)],
        scratch_shapes=[pltpu.VMEM((dv, dk), F32)],
        compiler_params=_params(("parallel", "parallel", "arbitrary")),
        name="ret_prompt",
    )(proj, proj, proj, proj, cos, sin, dq, dkk, dm, gc, gain.reshape(1, dv))


def _col(row):
    n = row.shape[1]
    return jnp.transpose(jnp.broadcast_to(row, (8, n)))[:, 0:1]


def _even_sample_kernel(p_ref, sh_ref, sr_ref, lb_ref, hg_ref, rg_ref, cos_ref, sin_ref, gam_ref,
                        mix_ref, sho_ref, sro_ref):
    hh, dk, dv = HGRN_HEADS, HGRN_DK, HGRN_DV
    for h in range(hh):
        def grp(i, h=h):
            return p_ref[:, (i * hh + h) * dk:(i * hh + h + 1) * dk]
        q, z, v, g = grp(0), grp(1), grp(2), grp(3)
        kin = (1.0 - lb_ref[:, h * dk:(h + 1) * dk]) * jax.nn.sigmoid(-z)
        s_new = _col(1.0 - kin) * sh_ref[h] + _col(kin) * v
        sho_ref[h] = s_new
        o = jnp.sum(_col(q) * s_new, axis=0, keepdims=True)
        mix_ref[:, h * dv:(h + 1) * dv] = _rms(o) * hg_ref[...] * _silu(g)
    base = 4 * hh * dk
    mix0 = hh * dv
    rh, rk, rv = RET_HEADS, RET_DK, RET_DV
    half = rk // 2
    cos = cos_ref[...]
    sin = sin_ref[...]

    def rot(x):
        x1, x2 = x[:, :half], x[:, half:]
        return jnp.concatenate([x1 * cos - x2 * sin, x1 * sin + x2 * cos], axis=1)

    for h in range(rh):
        def grp(i, h=h):
            return p_ref[:, base + (i * rh + h) * rk:base + (i * rh + h + 1) * rk]
        qr = rot(grp(0))
        kr = rot(grp(1)) * (rk ** -0.5)
        s_new = gam_ref[:, h:h + 1] * sr_ref[h] + _col(kr) * grp(2)
        sro_ref[h] = s_new
        o = jnp.sum(_col(qr) * s_new, axis=0, keepdims=True)
        mix_ref[:, mix0 + h * rv:mix0 + (h + 1) * rv] = _rms(o) * rg_ref[...] * _silu(grp(3))


def even_sample(proj_s, state_hgrn, state_ret, lb, hgrn_gain, ret_gain, pos):
    s, n_in = proj_s.shape
    n_mix = HGRN_HEADS * HGRN_DV + RET_HEADS * RET_DV
    cos, sin = _rope_tables(jnp.full((1,), pos), RET_DK // 2)
    gam = jnp.zeros((1, LANES), F32).at[0, :RET_HEADS].set(jnp.exp(_ret_log_gamma()))
    full = lambda shape: pl.BlockSpec(shape, lambda b: (0,) * len(shape))
    hs = pl.BlockSpec((None, HGRN_HEADS, HGRN_DK, HGRN_DV), lambda b: (b, 0, 0, 0))
    rs = pl.BlockSpec((None, RET_HEADS, RET_DK, RET_DV), lambda b: (b, 0, 0, 0))
    mix, sh, sr = pl.pallas_call(
        _even_sample_kernel,
        grid=(s,),
        in_specs=[pl.BlockSpec((None, 1, n_in), lambda b: (b, 0, 0)), hs, rs,
                  full((1, HGRN_HEADS * HGRN_DK)), full((1, HGRN_DV)), full((1, RET_DV)),
                  full((1, RET_DK // 2)), full((1, RET_DK // 2)), full((1, LANES))],
        out_specs=[pl.BlockSpec((None, 1, n_mix), lambda b: (b, 0, 0)), hs, rs],
        out_shape=[jax.ShapeDtypeStruct((s, 1, n_mix), F32),
                   jax.ShapeDtypeStruct(state_hgrn.shape, F32),
                   jax.ShapeDtypeStruct(state_ret.shape, F32)],
        compiler_params=_params(("parallel",)),
        name="even_sample",
    )(proj_s.reshape(s, 1, n_in), state_hgrn, state_ret, lb.reshape(1, -1), hgrn_gain.reshape(1, -1),
      ret_gain.reshape(1, -1), cos, sin, gam)
    return mix.reshape(s, n_mix), sh, sr


def _block_rank(gm, lane, nblk):
    rank = jnp.zeros(gm.shape, jnp.int32)
    for j in range(nblk):
        cj = gm[:, j:j + 1]
        beats = jnp.logical_or(cj > gm, jnp.logical_and(cj == gm, lane > j))
        rank = rank + jnp.where(beats, 1, 0)
    return rank


def _moba_kernel(q_ref, k_ref, v_ref, o_ref, kbf, vbf, km, *, tq, nblk):
    qi = pl.program_id(2)
    d = MOBA_DIM

    @pl.when(qi == 0)
    def _():
        kbf[...] = k_ref[...].astype(BF16)
        vbf[...] = v_ref[...].astype(BF16)
        km[...] = jnp.zeros_like(km)
        for j in range(nblk):
            km[j:j + 1, :] = jnp.mean(k_ref[j * tq:(j + 1) * tq, :], axis=0, keepdims=True)

    q = q_ref[...]
    gate = _dot3(q, km[...], _dot_nt)
    lane = lax.broadcasted_iota(jnp.int32, (tq, LANES), 1)
    gm = jnp.where(lane < qi, gate, -jnp.inf)
    sel = jnp.where(jnp.logical_and(lane < qi, _block_rank(gm, lane, nblk) < MOBA_TOPK), 1.0, 0.0)
    qs = (q * (d ** -0.5)).astype(BF16)
    row = lax.broadcasted_iota(jnp.int32, (tq, tq), 0)
    col = lax.broadcasted_iota(jnp.int32, (tq, tq), 1)
    causal = jnp.where(col <= row, 1.0, 0.0)

    def body(it, carry):
        m, l, acc = carry
        jj = qi - it
        off = pl.multiple_of(jj * tq, tq)
        s = _dot_nt(qs, kbf[pl.ds(off, tq), :])
        selcol = jnp.sum(jnp.where(lane == jj, sel, 0.0), axis=1, keepdims=True)
        keep = jnp.where(it == 0, causal, jnp.broadcast_to(selcol, (tq, tq)))
        s = jnp.where(keep > 0.5, s, NEG)
        m_new = jnp.maximum(m, jnp.max(s, axis=1, keepdims=True))
        alpha = jnp.exp(m - m_new)
        p = jnp.exp(s - m_new)
        l = alpha * l + jnp.sum(p, axis=1, keepdims=True)
        acc = alpha * acc + _dot(p.astype(BF16), vbf[pl.ds(off, tq), :])
        return m_new, l, acc

    init = (jnp.full((tq, 1), NEG, F32), jnp.zeros((tq, 1), F32), jnp.zeros((tq, d), F32))
    _, l, acc = lax.fori_loop(0, qi + 1, body, init)
    o_ref[...] = (acc / l).astype(o_ref.dtype)


def moba_prompt(proj, batch, seq):
    h, d, tq = MOBA_HEADS, MOBA_DIM, MOBA_BLOCK
    assert seq % tq == 0
    nq = seq // tq
    return pl.pallas_call(
        functools.partial(_moba_kernel, tq=tq, nblk=nq),
        grid=(batch, h, nq),
        in_specs=[pl.BlockSpec((tq, d), lambda b, hh, i: (b * nq + i, hh)),
                  pl.BlockSpec((seq, d), lambda b, hh, i: (b, h + hh)),
                  pl.BlockSpec((seq, d), lambda b, hh, i: (b, 2 * h + hh))],
        out_specs=pl.BlockSpec((tq, d), lambda b, hh, i: (b * nq + i, hh)),
        out_shape=jax.ShapeDtypeStruct((batch * seq, h * d), BF16),
        scratch_shapes=[pltpu.VMEM((seq, d), BF16), pltpu.VMEM((seq, d), BF16), pltpu.VMEM((LANES, d), F32)],
        compiler_params=_params(("parallel", "parallel", "arbitrary")),
        name="moba_prompt",
    )(proj, proj, proj)


def _diff_kernel(lam_ref, q_ref, k_ref, v_ref, gain_ref, o_ref, kbf, vbf, *, tq, out_scale):
    qi = pl.program_id(2)
    d = DIFF_DIM

    @pl.when(qi == 0)
    def _():
        kbf[...] = k_ref[...].astype(BF16)
        vbf[...] = v_ref[...].astype(BF16)

    q = (q_ref[...] * (d ** -0.5)).astype(BF16)
    row = lax.broadcasted_iota(jnp.int32, (tq, tq), 0)
    col = lax.broadcasted_iota(jnp.int32, (tq, tq), 1)
    causal = jnp.where(col <= row, 1.0, 0.0)

    def body(it, carry):
        jj = qi - it
        off = pl.multiple_of(jj * tq, tq)
        kj = kbf[pl.ds(off, tq), :]
        vj = vbf[pl.ds(off, tq), :]
        keep = jnp.where(it == 0, causal, 1.0) > 0.5
        out = []
        for mi in range(2):
            m, l, acc = carry[mi]
            s = jnp.where(keep, _dot_nt(q[:, mi * d:(mi + 1) * d], kj[:, mi * d:(mi + 1) * d]), NEG)
            m_new = jnp.maximum(m, jnp.max(s, axis=1, keepdims=True))
            alpha = jnp.exp(m - m_new)
            p = jnp.exp(s - m_new)
            out.append((m_new, alpha * l + jnp.sum(p, axis=1, keepdims=True), alpha * acc + _dot(p.astype(BF16), vj)))
        return tuple(out)

    one = (jnp.full((tq, 1), NEG, F32), jnp.zeros((tq, 1), F32), jnp.zeros((tq, DIFF_DV), F32))
    (_, l0, a0), (_, l1, a1) = lax.fori_loop(0, qi + 1, body, (one, one))
    o = a0 / l0 - lam_ref[0] * (a1 / l1)
    o_ref[...] = (_rms(o) * gain_ref[...] * out_scale).astype(o_ref.dtype)


def diff_prompt(proj, lam, gain, batch, seq, col0, out_scale, *, tq=256):
    h, dv = DIFF_HEADS, DIFF_DV
    tq = min(tq, seq)
    nq = seq // tq
    base = col0 // dv
    return pl.pallas_call(
        functools.partial(_diff_kernel, tq=tq, out_scale=out_scale),
        grid=(batch, h, nq),
        in_specs=[pl.BlockSpec(memory_space=pltpu.SMEM),
                  pl.BlockSpec((tq, dv), lambda b, hh, i: (b * nq + i, base + hh)),
                  pl.BlockSpec((seq, dv), lambda b, hh, i: (b, base + h + hh)),
                  pl.BlockSpec((seq, dv), lambda b, hh, i: (b, base + 2 * h + hh)),
                  pl.BlockSpec((1, dv), lambda b, hh, i: (0, 0))],
        out_specs=pl.BlockSpec((tq, dv), lambda b, hh, i: (b * nq + i, hh)),
        out_shape=jax.ShapeDtypeStruct((batch * seq, h * dv), BF16),
        scratch_shapes=[pltpu.VMEM((seq, dv), BF16), pltpu.VMEM((seq, dv), BF16)],
        compiler_params=_params(("parallel", "parallel", "arbitrary")),
        name="diff_prompt",
    )(lam.reshape(1), proj, proj, proj, gain.reshape(1, dv))


def _block_diag_rows(x8, nrow):
    rid = lax.broadcasted_iota(jnp.int32, x8.shape, 0)
    return jnp.concatenate([jnp.where(rid == r, x8, 0.0) for r in range(nrow)], axis=1)


def _moba_s_kernel(pt_ref, q_ref, kn_ref, vn_ref, *refs, npages):
    del pt_ref
    k_refs, v_refs, o_ref = refs[:npages], refs[npages:2 * npages], refs[2 * npages]
    hh, d = MOBA_HEADS, MOBA_DIM
    ppb = MOBA_BLOCK // PAGE_SIZE
    nblk = npages // ppb
    scale = d ** -0.5
    q8 = q_ref[...]
    qbd = _block_diag_rows(q8, hh)
    qbd_s = (qbd * scale).astype(BF16)
    scores, ksum = [], []
    for i in range(npages):
        kp = k_refs[i][...]
        scores.append(_dot_nt(qbd_s, kp.astype(BF16)))
        ksum.append(jnp.sum(kp, axis=0, keepdims=True))
    s_all = jnp.concatenate(scores, axis=1)
    kmean = [sum(ksum[j * ppb:(j + 1) * ppb]) * (1.0 / MOBA_BLOCK) for j in range(nblk)]
    km = jnp.concatenate(kmean + [jnp.zeros((LANES - nblk, hh * d), F32)], axis=0)
    gate = _dot3(qbd, km, _dot_nt)
    lane = lax.broadcasted_iota(jnp.int32, (hh, LANES), 1)
    gm = jnp.where(lane < nblk, gate, -jnp.inf)
    sel = jnp.where(jnp.logical_and(lane < nblk, _block_rank(gm, lane, nblk) < MOBA_TOPK), 1.0, 0.0)
    past = npages * PAGE_SIZE
    expand = jnp.where(lax.broadcasted_iota(jnp.int32, (LANES, past), 1) // MOBA_BLOCK
                       == lax.broadcasted_iota(jnp.int32, (LANES, past), 0), 1.0, 0.0).astype(BF16)
    keep = _dot(sel.astype(BF16), expand) > 0.5
    s_all = jnp.where(keep, s_all, NEG)
    s_new = jnp.sum(q8 * kn_ref[...], axis=1, keepdims=True) * scale
    m = jnp.maximum(jnp.max(s_all, axis=1, keepdims=True), s_new)
    p = jnp.exp(s_all - m)
    p_new = jnp.exp(s_new - m)
    l = jnp.sum(p, axis=1, keepdims=True) + p_new
    pb = p.astype(BF16)
    acc = jnp.zeros((hh, hh * d), F32)
    for i in range(npages):
        acc = acc + _dot(pb[:, i * PAGE_SIZE:(i + 1) * PAGE_SIZE], v_refs[i][...].astype(BF16))
    rid = lax.broadcasted_iota(jnp.int32, (hh, d), 0)
    o8 = jnp.zeros((hh, d), F32)
    for h in range(hh):
        o8 = o8 + jnp.where(rid == h, acc[:, h * d:(h + 1) * d], 0.0)
    o_ref[...] = (o8 + p_new * vn_ref[...]) / l


def _diff_s_kernel(pt_ref, lam_ref, q_ref, kn_ref, vn_ref, gain_ref, *refs, npages, out_scale):
    del pt_ref
    k_refs, v_refs, o_ref = refs[:npages], refs[npages:2 * npages], refs[2 * npages]
    hh, d, dv = DIFF_HEADS, DIFF_DIM, DIFF_DV
    nrow = 2 * hh
    q8 = q_ref[...] * (d ** -0.5)
    qbd = _block_diag_rows(q8, nrow).astype(BF16)
    s_all = jnp.concatenate([_dot_nt(qbd, k_refs[i][...].astype(BF16)) for i in range(npages)], axis=1)
    s_new = jnp.sum(q8 * kn_ref[...], axis=1, keepdims=True)
    m = jnp.maximum(jnp.max(s_all, axis=1, keepdims=True), s_new)
    p = jnp.exp(s_all - m)
    p_new = jnp.exp(s_new - m)
    l = jnp.sum(p, axis=1, keepdims=True) + p_new
    rid = lax.broadcasted_iota(jnp.int32, (nrow, 1), 0)
    coef = jnp.where(rid % 2 == 0, 1.0, -lam_ref[0]) / l
    w = (p * coef).astype(BF16)
    w_new = p_new * coef
    acc = jnp.zeros((nrow, hh * dv), F32)
    for i in range(npages):
        acc = acc + _dot(w[:, i * PAGE_SIZE:(i + 1) * PAGE_SIZE], v_refs[i][...].astype(BF16))
    vn = vn_ref[...]
    rows = []
    for h in range(hh):
        a = acc[2 * h:2 * h + 1, h * dv:(h + 1) * dv] + acc[2 * h + 1:2 * h + 2, h * dv:(h + 1) * dv]
        rows.append(a + (w_new[2 * h:2 * h + 1, :] + w_new[2 * h + 1:2 * h + 2, :]) * vn[h:h + 1, :])
    o = jnp.concatenate(rows, axis=0)
    o_ref[...] = _rms(o) * gain_ref[...] * out_scale


def _paged_specs(npages, width):
    def spec(i):
        return pl.BlockSpec((None, PAGE_SIZE, width), lambda b, pt: (pt[b, i], 0, 0))
    return [spec(i) for i in range(npages)]


def moba_sample(qn, kn, vn, cache_k, cache_v, page_table):
    s, npages = page_table.shape
    hh, d = MOBA_HEADS, MOBA_DIM
    assert (npages * PAGE_SIZE) % MOBA_BLOCK == 0
    tok = pl.BlockSpec((None, hh, d), lambda b, pt: (b, 0, 0))
    return pl.pallas_call(
        functools.partial(_moba_s_kernel, npages=npages),
        grid_spec=pltpu.PrefetchScalarGridSpec(
            num_scalar_prefetch=1, grid=(s,),
            in_specs=[tok, tok, tok] + _paged_specs(npages, hh * d) + _paged_specs(npages, hh * d),
            out_specs=tok),
        out_shape=jax.ShapeDtypeStruct((s, hh, d), F32),
        compiler_params=_params(("parallel",)),
        name="moba_sample",
    )(page_table, qn, kn, vn, *([cache_k] * npages), *([cache_v] * npages))


def diff_sample(qn, kn, vn, cache_k, cache_v, page_table, lam, gain, out_scale):
    s, npages = page_table.shape
    hh, d, dv = DIFF_HEADS, DIFF_DIM, DIFF_DV
    tokq = pl.BlockSpec((None, 2 * hh, d), lambda b, pt: (b, 0, 0))
    tokv = pl.BlockSpec((None, hh, dv), lambda b, pt: (b, 0, 0))
    return pl.pallas_call(
        functools.partial(_diff_s_kernel, npages=npages, out_scale=out_scale),
        grid_spec=pltpu.PrefetchScalarGridSpec(
            num_scalar_prefetch=1, grid=(s,),
            in_specs=[pl.BlockSpec(memory_space=pltpu.SMEM), tokq, tokq, tokv,
                      pl.BlockSpec((1, dv), lambda b, pt: (0, 0))]
            + _paged_specs(npages, 2 * hh * d) + _paged_specs(npages, hh * dv),
            out_specs=tokv),
        out_shape=jax.ShapeDtypeStruct((s, hh, dv), F32),
        compiler_params=_params(("parallel",)),
        name="diff_sample",
    )(page_table, lam.reshape(1), qn, kn, vn, gain.reshape(1, dv), *([cache_k] * npages), *([cache_v] * npages))


def _router_kernel(x_ref, g_ref, w_ref, b_ref, o_ref):
    ng, eg = MOE_GROUPS, MOE_EXPERTS_PER_GROUP
    h = _rms(x_ref[...]) * g_ref[...]
    logits = _dot3(h, w_ref[...]) + b_ref[...]
    lane = lax.broadcasted_iota(jnp.int32, logits.shape, 1)
    big = jnp.int32(1 << 20)

    def first_max(x):
        mx = jnp.max(x, axis=1, keepdims=True)
        return mx, jnp.min(jnp.where(x == mx, lane, big), axis=1, keepdims=True)

    gl = jnp.where(lane < ng, logits, -jnp.inf)
    gmax, gsel = first_max(gl)
    p_g = 1.0 / jnp.sum(jnp.exp(gl - gmax), axis=1, keepdims=True)
    lo = ng + gsel * eg
    el = jnp.where(jnp.logical_and(lane >= lo, lane < lo + eg), logits, -jnp.inf)
    m1, i1 = first_max(el)
    m2, i2 = first_max(jnp.where(lane == i1, -jnp.inf, el))
    e2 = jnp.exp(m2 - m1)
    g1 = p_g / (1.0 + e2)
    out = jnp.where(lane == 0, (i1 - ng).astype(F32),
                    jnp.where(lane == 1, (i2 - ng).astype(F32),
                              jnp.where(lane == 2, g1, jnp.where(lane == 3, g1 * e2, 0.0))))
    o_ref[...] = out


def moe_route(x, gain, w_rg, b_rg, w_re, b_re):
    t, d = x.shape
    ng, ne = MOE_GROUPS, MOE_EXPERTS
    w = jnp.zeros((d, LANES), F32).at[:, :ng].set(w_rg).at[:, ng:ng + ne].set(w_re)
    b = jnp.zeros((1, LANES), F32).at[0, :ng].set(b_rg).at[0, ng:ng + ne].set(b_re.reshape(ne))
    tm = _row_tile(t, 1024)
    return pl.pallas_call(
        _router_kernel,
        grid=(t // tm,),
        in_specs=[pl.BlockSpec((tm, d), lambda i: (i, 0)), pl.BlockSpec((1, d), lambda i: (0, 0)),
                  pl.BlockSpec((d, LANES), lambda i: (0, 0)), pl.BlockSpec((1, LANES), lambda i: (0, 0))],
        out_specs=pl.BlockSpec((tm, LANES), lambda i: (i, 0)),
        out_shape=jax.ShapeDtypeStruct((t, LANES), F32),
        compiler_params=_params(("parallel",)),
        name="moe_route",
    )(x, gain.reshape(1, d), w, b)


def _moe_kernel(blk_e_ref, tok_ref, dst_ref, nvalid_ref, nused_ref, x_hbm, gain_ref, wg_ref, wu_ref, wd_ref, out_hbm,
                xbuf, obuf, gsem, ssem, *, bm):
    del blk_e_ref
    i = pl.program_id(0)
    nused = nused_ref[0]
    slot = i % 2

    def gather_start(blk, s):
        def body(r, c):
            t = tok_ref[blk * bm + r]
            pltpu.make_async_copy(x_hbm.at[pl.ds(t, 1), :], xbuf.at[s, pl.ds(r, 1), :], gsem.at[s]).start()
            return c
        lax.fori_loop(0, bm, body, 0)

    def scatter_start(blk, s):
        def body(r, c):
            t = dst_ref[blk * bm + r]
            pltpu.make_async_copy(obuf.at[s, pl.ds(r, 1), :], out_hbm.at[pl.ds(t, 1), :], ssem.at[s]).start()
            return c
        lax.fori_loop(0, nvalid_ref[blk], body, 0)

    @pl.when(i == 0)
    def _():
        gather_start(0, 0)

    @pl.when(i < nused)
    def _():
        @pl.when(i + 1 < nused)
        def _():
            gather_start(i + 1, 1 - slot)

        pltpu.make_async_copy(x_hbm.at[pl.ds(0, bm), :], xbuf.at[slot], gsem.at[slot]).wait()
        x = (_rms(xbuf[slot]) * gain_ref[...]).astype(BF16)
        hid = (_silu(_dot(x, wg_ref[...])) * _dot(x, wu_ref[...])).astype(BF16)
        obuf[slot] = _dot(hid, wd_ref[...])
        scatter_start(i, slot)

    @pl.when(jnp.logical_and(i >= 1, i <= nused))
    def _():
        def body(r, c):
            pltpu.make_async_copy(obuf.at[1 - slot, pl.ds(0, 1), :], out_hbm.at[pl.ds(0, 1), :], ssem.at[1 - slot]).wait()
            return c
        lax.fori_loop(0, nvalid_ref[i - 1], body, 0)


def moe_experts(x, gain, route, w_gate, w_up, w_down):
    t, d = x.shape
    ne, _, ff = w_gate.shape
    kk, bm = MOE_TOPK, MOE_BLOCK
    a = t * kk
    assert a % bm == 0
    nb = a // bm + ne
    p = nb * bm
    flat_e = route[:, :kk].astype(jnp.int32).reshape(a)
    onehot = (flat_e[:, None] == jnp.arange(ne, dtype=jnp.int32)[None, :]).astype(jnp.int32)
    csum = jnp.cumsum(onehot, axis=0)
    rank = jnp.sum(csum * onehot, axis=1) - 1
    counts = csum[-1]
    padded = (counts + bm - 1) // bm * bm
    pend = jnp.cumsum(padded)
    dest = (pend - padded)[flat_e] + rank
    aidx = jnp.arange(a, dtype=jnp.int32)
    tok_buf = jnp.zeros((p,), jnp.int32).at[dest].set(aidx // kk, unique_indices=True)
    dst_buf = jnp.zeros((p,), jnp.int32).at[dest].set((aidx % kk) * t + aidx // kk, unique_indices=True)
    blk_start = jnp.arange(nb, dtype=jnp.int32) * bm
    blk_e = jnp.minimum(jnp.sum(pend[None, :] <= blk_start[:, None], axis=1), ne - 1).astype(jnp.int32)
    nvalid = jnp.clip(counts[blk_e] - (blk_start - (pend - padded)[blk_e]), 0, bm).astype(jnp.int32)
    nused = (pend[-1] // bm).astype(jnp.int32).reshape(1)
    nvalid = jnp.where(jnp.arange(nb) < nused[0], nvalid, 0)
    wspec = lambda shape: pl.BlockSpec((None,) + shape, lambda i, be, tk, ds, nv, nu: (be[i], 0, 0))
    return pl.pallas_call(
        functools.partial(_moe_kernel, bm=bm),
        grid_spec=pltpu.PrefetchScalarGridSpec(
            num_scalar_prefetch=5, grid=(nb,),
            in_specs=[pl.BlockSpec(memory_space=pl.ANY),
                      pl.BlockSpec((1, d), lambda i, be, tk, ds, nv, nu: (0, 0)),
                      wspec((d, ff)), wspec((d, ff)), wspec((ff, d))],
            out_specs=pl.BlockSpec(memory_space=pl.ANY),
            scratch_shapes=[pltpu.VMEM((2, bm, d), F32), pltpu.VMEM((2, bm, d), F32),
                            pltpu.SemaphoreType.DMA((2,)), pltpu.SemaphoreType.DMA((2,))]),
        out_shape=jax.ShapeDtypeStruct((kk * t, d), F32),
        compiler_params=_params(("arbitrary",)),
        name="moe_experts",
    )(blk_e, tok_buf, dst_buf, nvalid, nused, x, gain.reshape(1, d), w_gate, w_up, w_down)


def _combine_kernel(x_ref, r_ref, o0_ref, o1_ref, y_ref):
    r = r_ref[...]
    y_ref[...] = x_ref[...] + r[:, 2:3] * o0_ref[...] + r[:, 3:4] * o1_ref[...]


def moe_combine(x, route, out2):
    t, d = x.shape
    tm = _row_tile(t, 1024)
    nt = t // tm
    return pl.pallas_call(
        _combine_kernel,
        grid=(nt,),
        in_specs=[pl.BlockSpec((tm, d), lambda i: (i, 0)), pl.BlockSpec((tm, LANES), lambda i: (i, 0)),
                  pl.BlockSpec((tm, d), lambda i: (i, 0)), pl.BlockSpec((tm, d), lambda i: (nt + i, 0))],
        out_specs=pl.BlockSpec((tm, d), lambda i: (i, 0)),
        out_shape=jax.ShapeDtypeStruct((t, d), F32),
        compiler_params=_params(("parallel",)),
        name="moe_combine",
    )(x, route, out2, out2)


def hier_moe_residual(x, gain, w_rg, b_rg, w_re, b_re, w_gate, w_up, w_down):
    route = moe_route(x, gain, w_rg, b_rg, w_re, b_re)
    out2 = moe_experts(x, gain, route, w_gate, w_up, w_down)
    return moe_combine(x, route, out2)


def kernel(x_prompt, x_sample, state_hgrn, state_ret, cache_moba_k, cache_moba_v, cache_diff_k, cache_diff_v,
           page_table, norm_mix, norm_ffn, w_in_even, w_out_even, hgrn_lb_logits, hgrn_out_norm, ret_out_norm,
           w_in_odd, w_out_odd, moba_q_norm, moba_k_norm, diff_q_norm, diff_k_norm,
           diff_lambda_q1, diff_lambda_k1, diff_lambda_q2, diff_lambda_k2, diff_out_norm,
           router_group_w, router_group_b, router_expert_w, router_expert_b,
           expert_w_gate, expert_w_up, expert_w_down):
    bp, seq, d = x_prompt.shape
    bs = x_sample.shape[0]
    tp = bp * seq
    past_len = page_table.shape[1] * PAGE_SIZE
    x = jnp.concatenate([x_prompt.reshape(tp, d), x_sample.reshape(bs, d)], axis=0)
    lb_all = jnp.cumsum(jax.nn.softmax(hgrn_lb_logits.astype(F32), axis=0), axis=0)

    def moe(x, layer):
        return hier_moe_residual(x, norm_ffn[layer], router_group_w[layer], router_group_b[layer],
                                 router_expert_w[layer], router_expert_b[layer],
                                 expert_w_gate[layer].astype(BF16), expert_w_up[layer].astype(BF16),
                                 expert_w_down[layer].astype(BF16))

    h = rmsnorm_rows(x, norm_mix[0])
    proj = matmul(h, w_in_even[0].astype(BF16))
    ret_col0 = 4 * HGRN_HEADS * HGRN_DK
    mix_a, hg_p = hgrn_prompt(proj, lb_all[0], hgrn_out_norm[0], bp, seq)
    mix_b, rt_p = ret_prompt(proj, ret_out_norm[0], bp, seq, ret_col0)
    mix_s, hg_s, rt_s = even_sample(proj[tp:], state_hgrn[0], state_ret[0], lb_all[0], hgrn_out_norm[0],
                                    ret_out_norm[0], past_len)
    mix = jnp.concatenate([jnp.concatenate([mix_a, mix_b], axis=1), mix_s.astype(BF16)], axis=0)
    x = matmul(mix, w_out_even[0].astype(BF16), residual=x)
    x = moe(x, 0)

    lam_init = 0.8 - 0.6 * math.exp(-0.3 * 1)
    lam = (jnp.exp(jnp.sum(diff_lambda_q1[0].astype(F32) * diff_lambda_k1[0].astype(F32)))
           - jnp.exp(jnp.sum(diff_lambda_q2[0].astype(F32) * diff_lambda_k2[0].astype(F32))) + lam_init)
    h = rmsnorm_rows(x, norm_mix[1])
    flags = jnp.array([1, 1, 0, 1, 1, 0], jnp.int32)
    ones = jnp.ones((LANES,), F32)
    gains = jnp.stack([moba_q_norm[0], moba_k_norm[0], ones, diff_q_norm[0], diff_k_norm[0], ones])
    proj = matmul_headnorm(h, w_in_odd[0].astype(BF16), flags, gains)
    mh, md = MOBA_HEADS, MOBA_DIM
    dh, dd, dv = DIFF_HEADS, DIFF_DIM, DIFF_DV
    sec = mh * md
    diff_col0 = 3 * sec
    mix_c = moba_prompt(proj, bp, seq)
    mix_d = diff_prompt(proj, lam, diff_out_norm[0], bp, seq, diff_col0, 1.0 - lam_init)
    ps = proj[tp:]
    npool = cache_moba_k.shape[1]
    o_c = moba_sample(ps[:, :sec].reshape(bs, mh, md), ps[:, sec:2 * sec].reshape(bs, mh, md),
                      ps[:, 2 * sec:3 * sec].reshape(bs, mh, md),
                      cache_moba_k[0].reshape(npool, PAGE_SIZE, sec), cache_moba_v[0].reshape(npool, PAGE_SIZE, sec),
                      page_table)
    o_d = diff_sample(ps[:, 3 * sec:4 * sec].reshape(bs, 2 * dh, dd), ps[:, 4 * sec:5 * sec].reshape(bs, 2 * dh, dd),
                      ps[:, 5 * sec:].reshape(bs, dh, dv),
                      cache_diff_k[0].reshape(npool, PAGE_SIZE, 2 * dh * dd),
                      cache_diff_v[0].reshape(npool, PAGE_SIZE, dh * dv), page_table, lam, diff_out_norm[0],
                      1.0 - lam_init)
    mix_s = jnp.concatenate([o_c.reshape(bs, sec), o_d.reshape(bs, dh * dv)], axis=1)
    mix = jnp.concatenate([jnp.concatenate([mix_c, mix_d], axis=1), mix_s.astype(BF16)], axis=0)
    x = matmul(mix, w_out_odd[0].astype(BF16), residual=x)
    x = moe(x, 1)

    pp, ss = proj[:tp], proj[tp:]
    return (x[:tp].reshape(bp, seq, d), x[tp:].reshape(bs, 1, d),
            hg_p[None], hg_s[None], rt_p[None], rt_s[None],
            pp[:, sec:2 * sec].reshape(1, bp, seq, mh, md), ss[:, sec:2 * sec].reshape(1, bs, 1, mh, md),
            pp[:, 2 * sec:3 * sec].reshape(1, bp, seq, mh, md), ss[:, 2 * sec:3 * sec].reshape(1, bs, 1, mh, md),
            pp[:, 4 * sec:5 * sec].reshape(1, bp, seq, dh, 2, dd), ss[:, 4 * sec:5 * sec].reshape(1, bs, 1, dh, 2, dd),
            pp[:, 5 * sec:].reshape(1, bp, seq, dh, dv), ss[:, 5 * sec:].reshape(1, bs, 1, dh, dv))
```
